```python
import jax, jax.numpy as jnp
from jax import lax
import numpy as np

D_MODEL = 2048
BATCH = 1
SEQ = 8192
DEPTH = 2
DEC_BATCH = 128
DEC_SEQ = 8
PAST_LEN = 16384
PAGE_SIZE = 128

N_A_LAYERS = DEPTH // 2
N_B_LAYERS = DEPTH - N_A_LAYERS
EPS = 1e-6
GDN_QK_HEADS = 16
GDN_V_HEADS = 32
GDN_DK = 128
GDN_DV = 128
CONV_WIDTH = 4
GDN_CHUNK = 64
GDN_QK_DIM = GDN_QK_HEADS * GDN_DK
GDN_V_DIM = GDN_V_HEADS * GDN_DV
GDN_CONV_DIM = 2 * GDN_QK_DIM + GDN_V_DIM
GDN_IN_DIM = GDN_CONV_DIM + GDN_V_DIM + 2 * GDN_V_HEADS
MLA_HEADS = 16
Q_RANK = 512
KV_RANK = 512
NOPE_DIM = 128
ROPE_DIM = 64
V_DIM = 128
ROPE_THETA = 10000.0
MLA_SCALE = (NOPE_DIM + ROPE_DIM) ** -0.5
Q_BLOCK = 128
N_GROUPS = 8
EXPERTS_PER_GROUP = 8
N_EXPERTS = N_GROUPS * EXPERTS_PER_GROUP
TOP_K_IN_GROUP = 2
D_EXPERT = 512
PLE_DIM = 256

kernel_name = 'yoco_gdn_mla_hmoe_decode_step'


def rmsnorm(x, g):
    xf = x.astype(jnp.float32)
    y = xf * lax.rsqrt(jnp.mean(xf * xf, axis=-1, keepdims=True) + EPS)
    return (y * g.astype(jnp.float32)).astype(x.dtype)


def l2norm(x):
    xf = x.astype(jnp.float32)
    return xf * lax.rsqrt(jnp.sum(xf * xf, axis=-1, keepdims=True) + EPS)


def rope_cos_sin(pos):
    half = ROPE_DIM // 2
    inv_freq = ROPE_THETA ** (-jnp.arange(half, dtype=jnp.float32) / half)
    ang = pos.astype(jnp.float32)[:, None] * inv_freq[None, :]
    return jnp.cos(ang), jnp.sin(ang)


def apply_rope(x, cos, sin):
    x1, x2 = jnp.split(x.astype(jnp.float32), 2, axis=-1)
    return jnp.concatenate([x1 * cos - x2 * sin, x2 * cos + x1 * sin], axis=-1).astype(x.dtype)


def causal_conv_silu(x, buf, w):
    T = x.shape[1]
    xe = jnp.concatenate([buf.astype(x.dtype), x], axis=1)
    y = sum(xe[:, j:j + T] * w[j] for j in range(CONV_WIDTH))
    return jax.nn.silu(y), xe[:, T:]


def gated_delta_rule(q, k, v, g, beta, S0):
    B, T, H, _ = q.shape
    DV = v.shape[-1]
    C = min(GDN_CHUNK, T)
    n = -(-T // C)
    pad = n * C - T

    def chunks(a):
        a = jnp.pad(a, [(0, 0), (0, pad)] + [(0, 0)] * (a.ndim - 2))
        a = a.reshape((B, n, C) + a.shape[2:])
        return jnp.transpose(a, (1, 0, 3, 2) + tuple(range(4, a.ndim)))

    qc, kc, vc, gc, bc = (chunks(a) for a in (q, k, v, g, beta))
    gc = jnp.cumsum(gc, axis=-1)
    idx = jnp.arange(C)
    causal = idx[:, None] >= idx[None, :]
    strict = idx[:, None] > idx[None, :]
    decay = jnp.exp(jnp.where(causal, gc[..., :, None] - gc[..., None, :], -jnp.inf))
    kb = kc * bc[..., None]
    A = jnp.where(strict, jnp.einsum('nbhcd,nbhsd->nbhcs', kb, kc) * decay, 0.0) + jnp.eye(C, dtype=jnp.float32)
    Wv = lax.linalg.triangular_solve(A, vc * bc[..., None], left_side=True, lower=True, unit_diagonal=True)
    Uk = lax.linalg.triangular_solve(A, kb * jnp.exp(gc)[..., None], left_side=True, lower=True, unit_diagonal=True)
    intra = jnp.einsum('nbhcd,nbhsd->nbhcs', qc, kc) * decay

    def step(S, xs):
        q_, k_, w_, u_, g_, a_ = xs
        v_new = w_ - jnp.einsum('bhcd,bhdv->bhcv', u_, S)
        o = jnp.einsum('bhcd,bhdv->bhcv', q_ * jnp.exp(g_)[..., None], S) + jnp.einsum('bhcs,bhsv->bhcv', a_, v_new)
        g_last = g_[..., -1:]
        S = S * jnp.exp(g_last)[..., None] + jnp.einsum('bhcd,bhcv->bhdv', k_ * jnp.exp(g_last - g_)[..., None], v_new)
        return S, o

    S, o = lax.scan(step, S0, (qc, kc, Wv, Uk, gc, intra))
    o = jnp.transpose(o, (1, 0, 3, 2, 4)).reshape(B, n * C, H, DV)[:, :T]
    return o, S


def gdn_mixer(xn, conv_buf, S0, w_in, conv_w, a_log, dt_bias, norm_g, w_out):
    B, T, _ = xn.shape
    proj = xn @ w_in
    z0 = GDN_CONV_DIM
    b0 = z0 + GDN_V_DIM
    a0 = b0 + GDN_V_HEADS
    qkv, new_buf = causal_conv_silu(proj[..., :z0], conv_buf, conv_w)
    z = proj[..., z0:b0].reshape(B, T, GDN_V_HEADS, GDN_DV)
    rep = GDN_V_HEADS // GDN_QK_HEADS
    q = jnp.repeat(l2norm(qkv[..., :GDN_QK_DIM].reshape(B, T, GDN_QK_HEADS, GDN_DK)), rep, axis=2) * (GDN_DK ** -0.5)
    k = jnp.repeat(l2norm(qkv[..., GDN_QK_DIM:2 * GDN_QK_DIM].reshape(B, T, GDN_QK_HEADS, GDN_DK)), rep, axis=2)
    v = qkv[..., 2 * GDN_QK_DIM:].reshape(B, T, GDN_V_HEADS, GDN_DV).astype(jnp.float32)
    beta = jax.nn.sigmoid(proj[..., b0:a0].astype(jnp.float32))
    g = -jnp.exp(a_log.astype(jnp.float32)) * jax.nn.softplus(proj[..., a0:].astype(jnp.float32) + dt_bias.astype(jnp.float32))
    o, S = gated_delta_rule(q, k, v, g, beta, S0.astype(jnp.float32))
    o = rmsnorm(o, norm_g) * jax.nn.silu(z.astype(jnp.float32))
    y = o.reshape(B, T, GDN_V_DIM).astype(xn.dtype) @ w_out
    return y, new_buf, S.astype(S0.dtype)


def mla_shared_kv(h, pos, g_kv_in, w_kva, g_ckv, g_kpe):
    kv = rmsnorm(h, g_kv_in) @ w_kva
    cos, sin = rope_cos_sin(pos)
    ckv = rmsnorm(kv[..., :KV_RANK], g_ckv)
    kpe = apply_rope(rmsnorm(kv[..., KV_RANK:], g_kpe), cos, sin)
    return ckv, kpe


def mla_keys(ckv, w_kvb, g_kn):
    return rmsnorm(jnp.einsum('...sr,rhd->...shd', ckv, w_kvb[..., :NOPE_DIM]), g_kn)


def mla_queries(xn, pos, w_qa, g_qa, w_qb, g_qn, g_qp):
    q = jnp.einsum('btr,rhe->bthe', rmsnorm(xn @ w_qa, g_qa), w_qb)
    cos, sin = rope_cos_sin(pos)
    q_nope = rmsnorm(q[..., :NOPE_DIM], g_qn)
    q_pe = apply_rope(rmsnorm(q[..., NOPE_DIM:], g_qp), cos[:, None, :], sin[:, None, :])
    return q_nope, q_pe


def mla_attend(q_nope, q_pe, k_nope, k_pe, ckv, w_uv, mask):
    s = (jnp.einsum('qhd,shd->hqs', q_nope, k_nope) + jnp.einsum('qhr,sr->hqs', q_pe, k_pe)).astype(jnp.float32) * MLA_SCALE
    p = jax.nn.softmax(jnp.where(mask, s, -jnp.inf), axis=-1).astype(ckv.dtype)
    latent = jnp.einsum('hqs,sr->qhr', p, ckv)
    return jnp.einsum('qhr,rhv->qhv', latent, w_uv)


def mla_prompt_attend(q_nope, q_pe, ckv, kpe, k_nope, w_uv):
    B, T = q_nope.shape[:2]
    nb = T // Q_BLOCK
    key_pos = jnp.arange(T)
    qn = jnp.swapaxes(q_nope.reshape(B, nb, Q_BLOCK, MLA_HEADS, NOPE_DIM), 0, 1)
    qp = jnp.swapaxes(q_pe.reshape(B, nb, Q_BLOCK, MLA_HEADS, ROPE_DIM), 0, 1)
    attend_b = jax.vmap(mla_attend, in_axes=(0, 0, 0, 0, 0, None, None))

    def block(args):
        blk, qn_b, qp_b = args
        q_pos = blk * Q_BLOCK + jnp.arange(Q_BLOCK)
        mask = key_pos[None, :] <= q_pos[:, None]
        return attend_b(qn_b, qp_b, k_nope, kpe, ckv, w_uv, mask)

    o = lax.map(block, (jnp.arange(nb), qn, qp))
    return jnp.swapaxes(o, 0, 1).reshape(B, T, MLA_HEADS, V_DIM)


def mla_paged_attend(q_nope, q_pe, ckv_new, kpe_new, cache_ckv, cache_kpe, page_table, w_kvb, g_kn):
    T = q_nope.shape[1]
    past = page_table.shape[1] * PAGE_SIZE
    w_uv = w_kvb[..., NOPE_DIM:]
    mask = jnp.concatenate([jnp.ones((T, past), bool), jnp.tril(jnp.ones((T, T), bool))], axis=1)

    def one_sequence(args):
        pages, qn, qp, cn, kn = args
        ckv_all = jnp.concatenate([cache_ckv[pages].reshape(past, KV_RANK), cn], axis=0)
        kpe_all = jnp.concatenate([cache_kpe[pages].reshape(past, ROPE_DIM), kn], axis=0)
        return mla_attend(qn, qp, mla_keys(ckv_all, w_kvb, g_kn), kpe_all, ckv_all, w_uv, mask)

    return lax.map(one_sequence, (page_table, q_nope, q_pe, ckv_new, kpe_new))


def hier_moe(xn, w_group, b_group, w_router, b_router, w_gate, w_up, w_down):
    B, T, D = xn.shape
    x = xn.reshape(B * T, D)
    n = x.shape[0]
    grp_logits = (x @ w_group).astype(jnp.float32) + b_group.astype(jnp.float32)
    grp = jnp.argmax(grp_logits, axis=-1)
    grp_w = jnp.take_along_axis(jax.nn.softmax(grp_logits, axis=-1), grp[:, None], axis=-1)
    exp_logits = ((x @ w_router).astype(jnp.float32) + b_router.astype(jnp.float32)).reshape(n, N_GROUPS, EXPERTS_PER_GROUP)
    in_grp = jnp.take_along_axis(exp_logits, grp[:, None, None], axis=1)[:, 0]
    top_v, top_i = lax.top_k(jax.nn.softmax(in_grp, axis=-1), TOP_K_IN_GROUP)
    top_v = top_v / jnp.sum(top_v, axis=-1, keepdims=True) * grp_w
    within = jnp.einsum('nk,nke->ne', top_v, jax.nn.one_hot(top_i, EXPERTS_PER_GROUP, dtype=jnp.float32))
    comb = (jax.nn.one_hot(grp, N_GROUPS, dtype=jnp.float32)[:, :, None] * within[:, None, :]).astype(x.dtype)
    y = jnp.zeros_like(x)
    for gi in range(N_GROUPS):
        e = slice(gi * EXPERTS_PER_GROUP, (gi + 1) * EXPERTS_PER_GROUP)
        hid = jax.nn.silu(jnp.einsum('nd,edf->nef', x, w_gate[e])) * jnp.einsum('nd,edf->nef', x, w_up[e])
        y = y + jnp.einsum('nef,ne,efd->nd', hid, comb[:, gi], w_down[e])
    return y.reshape(B, T, D)


def per_layer_embedding(h, p_i, g, w_gate, w_proj):
    return jax.nn.sigmoid(rmsnorm(h, g) @ w_gate) * (p_i.astype(h.dtype) @ w_proj)


def layer_stack(x, p, pos, conv_bufs, rec_states, prm, prepare_kv, attend):
    B, T, _ = x.shape
    h = x
    new_bufs, new_states = [], []
    ckv = kpe = shared = None
    for i in range(DEPTH):
        xn = rmsnorm(h, prm['g_attn'][i])
        if i < N_A_LAYERS:
            a, buf, S = gdn_mixer(xn, conv_bufs[i], rec_states[i], prm['w_gdn_in'][i], prm['gdn_conv_w'][i],
                                  prm['gdn_a_log'][i], prm['gdn_dt_bias'][i], prm['gdn_norm_g'][i], prm['w_gdn_out'][i])
            new_bufs.append(buf)
            new_states.append(S)
        else:
            j = i - N_A_LAYERS
            q_nope, q_pe = mla_queries(xn, pos, prm['w_qa'][j], prm['g_qa'][j], prm['w_qb'][j], prm['g_qn'][j], prm['g_qp'][j])
            a = attend(q_nope, q_pe, shared).reshape(B, T, MLA_HEADS * V_DIM) @ prm['w_mla_o'][j]
        h = h + a
        h = h + hier_moe(rmsnorm(h, prm['g_ffn'][i]), prm['w_group'][i], prm['b_group'][i], prm['w_router'][i],
                         prm['b_router'][i], prm['w_exp_gate'][i], prm['w_exp_up'][i], prm['w_exp_down'][i])
        h = h + per_layer_embedding(h, p[i], prm['g_ple'][i], prm['w_ple_gate'][i], prm['w_ple_proj'][i])
        if i == N_A_LAYERS - 1:
            ckv, kpe = mla_shared_kv(h, pos, prm['g_kv_in'], prm['w_kva'], prm['g_ckv'], prm['g_kpe'])
            shared = prepare_kv(ckv, kpe)
    return h, ckv, kpe, jnp.stack(new_bufs), jnp.stack(new_states)


def setup_inputs(seed: int = 0) -> dict:
    key = jax.random.key(seed)
    ks = list(jax.random.split(key, 48))
    f32 = jnp.float32

    def nrm(shape, scale):
        return jax.random.normal(ks.pop(), shape, f32) * scale

    def gain(shape):
        return 1.0 + nrm(shape, 0.02)

    n_pages = PAST_LEN // PAGE_SIZE
    n_used = DEC_BATCH * n_pages
    n_phys = n_used + max(1, n_used // 4)
    page_table = jax.random.permutation(ks.pop(), n_phys)[:n_used].reshape(DEC_BATCH, n_pages).astype(jnp.int32)
    a_log = jnp.log(jax.random.uniform(ks.pop(), (N_A_LAYERS, GDN_V_HEADS), f32, minval=1.0, maxval=16.0))
    dt = jnp.exp(jax.random.uniform(ks.pop(), (N_A_LAYERS, GDN_V_HEADS), f32, minval=float(np.log(1e-3)), maxval=float(np.log(1e-1))))
    dt_bias = dt + jnp.log(-jnp.expm1(-dt))
    D = D_MODEL
    return {
        'x_prompt': nrm((BATCH, SEQ, D), 1.0),
        'x_sample': nrm((DEC_BATCH, DEC_SEQ, D), 1.0),
        'cache_ckv': nrm((n_phys, PAGE_SIZE, KV_RANK), 1.0),
        'cache_kpe': nrm((n_phys, PAGE_SIZE, ROPE_DIM), 1.0),
        'state_gdn': nrm((N_A_LAYERS, DEC_BATCH, GDN_V_HEADS, GDN_DK, GDN_DV), 0.1),
        'state_conv': nrm((N_A_LAYERS, DEC_BATCH, CONV_WIDTH - 1, GDN_CONV_DIM), 1.0),
        'page_table': page_table,
        'p_prompt': nrm((DEPTH, BATCH, SEQ, PLE_DIM), 1.0),
        'p_sample': nrm((DEPTH, DEC_BATCH, DEC_SEQ, PLE_DIM), 1.0),
        'g_attn': gain((DEPTH, D)),
        'g_ffn': gain((DEPTH, D)),
        'g_ple': gain((DEPTH, D)),
        'w_gdn_in': nrm((N_A_LAYERS, D, GDN_IN_DIM), D ** -0.5),
        'gdn_conv_w': nrm((N_A_LAYERS, CONV_WIDTH, GDN_CONV_DIM), CONV_WIDTH ** -0.5),
        'gdn_a_log': a_log,
        'gdn_dt_bias': dt_bias,
        'gdn_norm_g': gain((N_A_LAYERS, GDN_DV)),
        'w_gdn_out': nrm((N_A_LAYERS, GDN_V_DIM, D), GDN_V_DIM ** -0.5),
        'g_kv_in': gain((D,)),
        'w_kva': nrm((D, KV_RANK + ROPE_DIM), D ** -0.5),
        'g_ckv': gain((KV_RANK,)),
        'g_kpe': gain((ROPE_DIM,)),
        'w_kvb': nrm((KV_RANK, MLA_HEADS, NOPE_DIM + V_DIM), KV_RANK ** -0.5),
        'g_kn': gain((NOPE_DIM,)),
        'w_qa': nrm((N_B_LAYERS, D, Q_RANK), D ** -0.5),
        'g_qa': gain((N_B_LAYERS, Q_RANK)),
        'w_qb': nrm((N_B_LAYERS, Q_RANK, MLA_HEADS, NOPE_DIM + ROPE_DIM), Q_RANK ** -0.5),
        'g_qn': gain((N_B_LAYERS, NOPE_DIM)),
        'g_qp': gain((N_B_LAYERS, ROPE_DIM)),
        'w_mla_o': nrm((N_B_LAYERS, MLA_HEADS * V_DIM, D), (MLA_HEADS * V_DIM) ** -0.5),
        'w_group': nrm((DEPTH, D, N_GROUPS), D ** -0.5),
        'b_group': nrm((DEPTH, N_GROUPS), 0.01),
        'w_router': nrm((DEPTH, D, N_EXPERTS), D ** -0.5),
        'b_router': nrm((DEPTH, N_EXPERTS), 0.01),
        'w_exp_gate': nrm((DEPTH, N_EXPERTS, D, D_EXPERT), D ** -0.5),
        'w_exp_up': nrm((DEPTH, N_EXPERTS, D, D_EXPERT), D ** -0.5),
        'w_exp_down': nrm((DEPTH, N_EXPERTS, D_EXPERT, D), D_EXPERT ** -0.5),
        'w_ple_gate': nrm((DEPTH, D, D), D ** -0.5),
        'w_ple_proj': nrm((DEPTH, PLE_DIM, D), PLE_DIM ** -0.5),
    }


def reference(x_prompt, x_sample, cache_ckv, cache_kpe, state_gdn, state_conv, page_table, p_prompt, p_sample,
              g_attn, g_ffn, g_ple, w_gdn_in, gdn_conv_w, gdn_a_log, gdn_dt_bias, gdn_norm_g, w_gdn_out,
              g_kv_in, w_kva, g_ckv, g_kpe, w_kvb, g_kn, w_qa, g_qa, w_qb, g_qn, g_qp, w_mla_o,
              w_group, b_group, w_router, b_router, w_exp_gate, w_exp_up, w_exp_down, w_ple_gate, w_ple_proj):
    prm = dict(g_attn=g_attn, g_ffn=g_ffn, g_ple=g_ple, w_gdn_in=w_gdn_in, gdn_conv_w=gdn_conv_w,
               gdn_a_log=gdn_a_log, gdn_dt_bias=gdn_dt_bias, gdn_norm_g=gdn_norm_g, w_gdn_out=w_gdn_out,
               g_kv_in=g_kv_in, w_kva=w_kva, g_ckv=g_ckv, g_kpe=g_kpe, w_qa=w_qa, g_qa=g_qa, w_qb=w_qb,
               g_qn=g_qn, g_qp=g_qp, w_mla_o=w_mla_o, w_group=w_group, b_group=b_group, w_router=w_router,
               b_router=b_router, w_exp_gate=w_exp_gate, w_exp_up=w_exp_up, w_exp_down=w_exp_down,
               w_ple_gate=w_ple_gate, w_ple_proj=w_ple_proj)
    w_uv = w_kvb[..., NOPE_DIM:]
    bp, tp = x_prompt.shape[0], x_prompt.shape[1]
    y_prompt, new_ckv_prompt, new_kpe_prompt, new_conv_prompt, new_gdn_prompt = layer_stack(
        x_prompt, p_prompt, jnp.arange(tp),
        jnp.zeros((N_A_LAYERS, bp, CONV_WIDTH - 1, GDN_CONV_DIM), x_prompt.dtype),
        jnp.zeros((N_A_LAYERS, bp, GDN_V_HEADS, GDN_DK, GDN_DV), state_gdn.dtype),
        prm,
        lambda c, k: (c, k, mla_keys(c, w_kvb, g_kn)),
        lambda qn, qp, sh: mla_prompt_attend(qn, qp, sh[0], sh[1], sh[2], w_uv))
    past_len = page_table.shape[1] * PAGE_SIZE
    y_sample, new_ckv_sample, new_kpe_sample, new_conv_sample, new_gdn_sample = layer_stack(
        x_sample, p_sample, past_len + jnp.arange(x_sample.shape[1]), state_conv, state_gdn, prm,
        lambda c, k: (c, k),
        lambda qn, qp, sh: mla_paged_attend(qn, qp, sh[0], sh[1], cache_ckv, cache_kpe, page_table, w_kvb, g_kn))
    return (y_prompt, y_sample, new_ckv_prompt, new_kpe_prompt, new_ckv_sample, new_kpe_sample,
            new_gdn_prompt, new_conv_prompt, new_gdn_sample, new_conv_sample)
```

```python
import functools
import math

import numpy as np
import jax
import jax.numpy as jnp
from jax import lax
from jax.experimental import pallas as pl
from jax.experimental.pallas import tpu as pltpu

F32 = jnp.float32
BF16 = jnp.bfloat16

EPS = 1e-6
ROPE_THETA = 10000.0
LANES = 128
SUBLANES = 8
VMEM_LIMIT = 48 * 1024 * 1024

GDN_CHUNK = 64
GDN_DK = 128
GDN_DV = 128
NOPE_DIM = 128
ROPE_DIM = 64
V_DIM = 128
PAGE_SIZE = 128
N_GROUPS = 8
EXPERTS_PER_GROUP = 8
NEG_BIG = -1e30


def _cparams(sem, vmem=VMEM_LIMIT):
    return pltpu.CompilerParams(dimension_semantics=sem, vmem_limit_bytes=vmem)


def _dot(a, b):
    return jnp.dot(a, b, preferred_element_type=F32)


def _dot_nt(a, b):
    return lax.dot_general(a, b, (((1,), (1,)), ((), ())), preferred_element_type=F32)


def _dot_tn(a, b):
    return lax.dot_general(a, b, (((0,), (0,)), ((), ())), preferred_element_type=F32)


def _split2(a):
    hi = a.astype(BF16)
    lo = (a - hi.astype(F32)).astype(BF16)
    return hi, lo


def _dot3(a, b, dot=_dot):
    ah, al = _split2(a)
    bh, bl = _split2(b)
    return dot(ah, bh) + (dot(ah, bl) + dot(al, bh))


def _rms(x, g):
    ms = jnp.mean(x * x, axis=-1, keepdims=True)
    return x * lax.rsqrt(ms + EPS) * g


def _sigmoid(x):
    return 1.0 / (1.0 + jnp.exp(-x))


def _silu(x):
    return x * _sigmoid(x)


def _mm_body(*refs, norm, has_res):
    it = iter(refs)
    x_ref = next(it)
    g_ref = next(it) if norm else None
    w_ref = next(it)
    res_ref = next(it) if has_res else None
    o_ref = next(it)
    if norm:
        xn_ref = next(it)

        @pl.when(pl.program_id(1) == 0)
        def _():
            xn_ref[...] = _rms(x_ref[...], g_ref[...]).astype(BF16)

        a = xn_ref[...]
    else:
        a = x_ref[...]
    acc = _dot(a, w_ref[...])
    if has_res:
        acc = acc + res_ref[...]
    o_ref[...] = acc.astype(o_ref.dtype)


def _mm(x, w, *, g=None, res=None, bm, bn, out_dtype=F32, name):
    M, K = x.shape
    N = w.shape[1]
    norm = g is not None
    in_specs = [pl.BlockSpec((bm, K), lambda i, j: (i, 0))]
    args = [x]
    if norm:
        in_specs.append(pl.BlockSpec((1, K), lambda i, j: (0, 0)))
        args.append(g.reshape(1, K))
    in_specs.append(pl.BlockSpec((K, bn), lambda i, j: (0, j)))
    args.append(w)
    if res is not None:
        in_specs.append(pl.BlockSpec((bm, bn), lambda i, j: (i, j)))
        args.append(res)
    scratch = [pltpu.VMEM((bm, K), BF16)] if norm else []
    return pl.pallas_call(
        functools.partial(_mm_body, norm=norm, has_res=res is not None),
        out_shape=jax.ShapeDtypeStruct((M, N), out_dtype),
        grid=(M // bm, N // bn),
        in_specs=in_specs,
        out_specs=pl.BlockSpec((bm, bn), lambda i, j: (i, j)),
        scratch_shapes=scratch,
        compiler_params=_cparams(("parallel", "arbitrary")),
        name=name,
    )(*args)


def _ple_body(x_ref, g_ref, wg_ref, p_ref, wp_ref, res_ref, o_ref, xn_ref):
    @pl.when(pl.program_id(1) == 0)
    def _():
        xn_ref[...] = _rms(x_ref[...], g_ref[...]).astype(BF16)

    gate = _dot(xn_ref[...], wg_ref[...])
    proj = _dot(p_ref[...].astype(BF16), wp_ref[...])
    o_ref[...] = res_ref[...] + _sigmoid(gate) * proj


def _ple(h, g, wg, p, wp, *, bm, bn, name):
    M, D = h.shape
    P = p.shape[1]
    return pl.pallas_call(
        _ple_body,
        out_shape=jax.ShapeDtypeStruct((M, D), F32),
        grid=(M // bm, D // bn),
        in_specs=[
            pl.BlockSpec((bm, D), lambda i, j: (i, 0)),
            pl.BlockSpec((1, D), lambda i, j: (0, 0)),
            pl.BlockSpec((D, bn), lambda i, j: (0, j)),
            pl.BlockSpec((bm, P), lambda i, j: (i, 0)),
            pl.BlockSpec((P, bn), lambda i, j: (0, j)),
            pl.BlockSpec((bm, bn), lambda i, j: (i, j)),
        ],
        out_specs=pl.BlockSpec((bm, bn), lambda i, j: (i, j)),
        scratch_shapes=[pltpu.VMEM((bm, D), BF16)],
        compiler_params=_cparams(("parallel", "arbitrary")),
        name=name,
    )(h, g.reshape(1, D), wg, p, wp, h)


def _bg_body(ba_ref, al_ref, dtb_ref, bb_ref, gcb_ref, gr_ref, *, C, nh):
    bm = ba_ref.shape[0]
    x = ba_ref[...]
    beta = _sigmoid(x)
    xa = x + dtb_ref[...]
    softplus = jnp.maximum(xa, 0.0) + jnp.log1p(jnp.exp(-jnp.abs(xa)))
    g = -jnp.exp(al_ref[...]) * softplus
    g1 = g.astype(BF16).astype(F32)
    r1 = g - g1
    g2 = r1.astype(BF16).astype(F32)
    g3 = (r1 - g2).astype(BF16).astype(F32)
    row = lax.broadcasted_iota(jnp.int32, (bm, bm), 0)
    col = lax.broadcasted_iota(jnp.int32, (bm, bm), 1)
    tril = jnp.where((row // C == col // C) & (col <= row), 1.0, 0.0).astype(BF16)
    gc = _dot(tril, g1.astype(BF16)) + (_dot(tril, g2.astype(BF16)) + _dot(tril, g3.astype(BF16)))
    for h in range(nh):
        bb_ref[h] = jnp.broadcast_to(beta[:, h:h + 1], (bm, LANES))
        gcb_ref[h] = jnp.broadcast_to(gc[:, nh + h:nh + h + 1], (bm, LANES))
    r = lax.broadcasted_iota(jnp.int32, (C, C), 0)
    c = lax.broadcasted_iota(jnp.int32, (C, C), 1)
    triu = jnp.where(r <= c, 1.0, 0.0).astype(BF16)
    for ch in range(bm // C):
        sl = slice(ch * C, (ch + 1) * C)
        gr_ref[ch] = (_dot_tn(g1[sl].astype(BF16), triu)
                      + (_dot_tn(g2[sl].astype(BF16), triu) + _dot_tn(g3[sl].astype(BF16), triu)))


def _bg(ba, a_log_row, dtb_row, *, row0, rows, C, bm, nh, name):
    nb = rows // bm
    b0 = row0 // bm
    return pl.pallas_call(
        functools.partial(_bg_body, C=C, nh=nh),
        out_shape=(
            jax.ShapeDtypeStruct((nh, rows, LANES), F32),
            jax.ShapeDtypeStruct((nh, rows, LANES), F32),
            jax.ShapeDtypeStruct((rows // C, LANES, C), F32),
        ),
        grid=(nb,),
        in_specs=[
            pl.BlockSpec((bm, LANES), lambda i: (b0 + i, 0)),
            pl.BlockSpec((1, LANES), lambda i: (0, 0)),
            pl.BlockSpec((1, LANES), lambda i: (0, 0)),
        ],
        out_specs=(
            pl.BlockSpec((nh, bm, LANES), lambda i: (0, i, 0)),
            pl.BlockSpec((nh, bm, LANES), lambda i: (0, i, 0)),
            pl.BlockSpec((bm // C, LANES, C), lambda i: (i, 0, 0)),
        ),
        compiler_params=_cparams(("parallel",)),
        name=name,
    )(ba, a_log_row, dtb_row)


def _conv_body(*refs, Bb, L, nh, normalize, scale, first_from_buf):
    it = iter(refs)
    x_ref = next(it)
    halo_ref = next(it)
    buf_ref = next(it) if first_from_buf else None
    w_ref = next(it)
    o_ref = next(it)
    cat_ref = next(it)
    bc = nh * LANES
    hl = halo_ref[...].reshape(Bb, SUBLANES, bc)
    if first_from_buf:
        hl = jnp.where(pl.program_id(0) == 0, buf_ref[...].reshape(Bb, SUBLANES, bc), hl)
    cat_ref[:, 0:SUBLANES, :] = hl
    cat_ref[:, SUBLANES:SUBLANES + L, :] = x_ref[...].reshape(Bb, L, bc)
    w = w_ref[...]
    y = None
    for j in range(4):
        term = cat_ref[:, SUBLANES - 3 + j:SUBLANES - 3 + j + L, :] * w[j:j + 1, :].reshape(1, 1, bc)
        y = term if y is None else y + term
    y = _silu(y).reshape(Bb * L, bc)
    for s in range(nh):
        yh = y[:, s * LANES:(s + 1) * LANES]
        if normalize:
            yh = yh * lax.rsqrt(jnp.sum(yh * yh, axis=-1, keepdims=True) + EPS)
        if scale != 1.0:
            yh = yh * scale
        o_ref[s] = yh


def _conv(proj, halo_src, conv_w, *, row0, rows, col0, cols, Bb, L, nh, normalize, scale, buf8=None, name):
    bc = nh * LANES
    tr = Bb * L
    nrt = rows // tr
    nct = cols // bc
    rb0 = row0 // tr
    cb0 = col0 // bc
    first_from_buf = buf8 is not None
    in_specs = [pl.BlockSpec((tr, bc), lambda i, j: (rb0 + i, cb0 + j))]
    args = [proj]
    if first_from_buf:
        hb0 = row0 // SUBLANES
        hstep = tr // SUBLANES
        in_specs.append(pl.BlockSpec((SUBLANES, bc), lambda i, j: (jnp.maximum(hb0 + i * hstep - 1, 0), cb0 + j)))
        args.append(proj)
        in_specs.append(pl.BlockSpec((SUBLANES, bc), lambda i, j: (0, j)))
        args.append(buf8)
    else:
        in_specs.append(pl.BlockSpec((Bb * SUBLANES, bc), lambda i, j: (i, j)))
        args.append(halo_src)
    in_specs.append(pl.BlockSpec((4, bc), lambda i, j: (0, j)))
    args.append(conv_w)
    return pl.pallas_call(
        functools.partial(_conv_body, Bb=Bb, L=L, nh=nh, normalize=normalize, scale=scale,
                          first_from_buf=first_from_buf),
        out_shape=jax.ShapeDtypeStruct((cols // LANES, rows, LANES), F32),
        grid=(nrt, nct),
        in_specs=in_specs,
        out_specs=pl.BlockSpec((nh, tr, LANES), lambda i, j: (j, i, 0)),
        scratch_shapes=[pltpu.VMEM((Bb, L + SUBLANES, bc), F32)],
        compiler_params=_cparams(("arbitrary", "arbitrary")),
        name=name,
    )(*args)


def _gdn_body(q_ref, k_ref, v_ref, bb_ref, gc_ref, gr_ref, z_ref, s0_ref, ng_ref, o_ref, so_ref, S_scr, o_scr,
              *, C, nvh, rep):
    c = pl.program_id(1)

    @pl.when(c == 0)
    def _():
        S_scr[...] = s0_ref[0]

    ri = lax.broadcasted_iota(jnp.int32, (C, C), 0)
    ci = lax.broadcasted_iota(jnp.int32, (C, C), 1)
    causal = ci <= ri
    strict = ci < ri
    eye = jnp.where(ri == ci, 1.0, 0.0).astype(F32)
    nlev = int(math.log2(C)) - 1

    def head(h, carry):
        hq = h // rep
        q = q_ref[hq]
        k = k_ref[hq]
        v = v_ref[h]
        beta = bb_ref[h]
        gc = gc_ref[h]
        grow = gr_ref[0, pl.ds(nvh + h, 1), :]
        decay = jnp.exp(jnp.where(causal, gc[:, :C] - grow, NEG_BIG))
        kb = k * beta
        kbf = k.astype(BF16)
        A = jnp.where(strict, _dot_nt(kb.astype(BF16), kbf) * decay, 0.0)
        T = eye - A
        P = A
        for _ in range(nlev):
            P = _dot3(P, P)
            T = T + _dot3(T, P)
        eg = jnp.exp(gc)
        WU = _dot3(T, jnp.concatenate([v * beta, kb * eg], axis=1))
        W = WU[:, :GDN_DV]
        U = WU[:, GDN_DV:]
        intra = jnp.where(causal, _dot_nt(q.astype(BF16), kbf) * decay, 0.0)
        S = S_scr[h]
        Sb = S.astype(BF16)
        v_new = W - _dot(U.astype(BF16), Sb)
        vnb = v_new.astype(BF16)
        o_scr[h] = _dot((q * eg).astype(BF16), Sb) + _dot(intra.astype(BF16), vnb)
        gl = gc[C - 1:C, :]
        kg = k * jnp.exp(gl - gc)
        S_scr[h] = S * jnp.exp(gl) + _dot_tn(kg.astype(BF16), vnb)
        return carry

    lax.fori_loop(0, nvh, head, 0)

    ng = ng_ref[...]
    for h in range(nvh):
        sl = slice(h * GDN_DV, (h + 1) * GDN_DV)
        o_ref[:, sl] = (_rms(o_scr[h], ng) * _silu(z_ref[:, sl])).astype(o_ref.dtype)

    @pl.when(c == pl.num_programs(1) - 1)
    def _():
        so_ref[0] = S_scr[...]


def _gdn(qh, kh, vh, bb, gcb, grow, proj, s0, norm_g, *, B, n, C, zrow0, out_dtype, name):
    nqh = qh.shape[0]
    nvh = vh.shape[0]
    rows = B * n * C
    zb0 = zrow0 // C
    vdim = nvh * GDN_DV
    zcb = (proj.shape[1] - vdim) // vdim
    hm = lambda b, c: (0, b * n + c, 0)
    return pl.pallas_call(
        functools.partial(_gdn_body, C=C, nvh=nvh, rep=nvh // nqh),
        out_shape=(
            jax.ShapeDtypeStruct((rows, vdim), out_dtype),
            jax.ShapeDtypeStruct((B, nvh, GDN_DK, GDN_DV), F32),
        ),
        grid=(B, n),
        in_specs=[
            pl.BlockSpec((nqh, C, LANES), hm),
            pl.BlockSpec((nqh, C, LANES), hm),
            pl.BlockSpec((nvh, C, LANES), hm),
            pl.BlockSpec((nvh, C, LANES), hm),
            pl.BlockSpec((nvh, C, LANES), hm),
            pl.BlockSpec((1, LANES, C), lambda b, c: (b * n + c, 0, 0)),
            pl.BlockSpec((C, vdim), lambda b, c: (zb0 + b * n + c, zcb)),
            pl.BlockSpec((1, nvh, GDN_DK, GDN_DV), lambda b, c: (b, 0, 0, 0)),
            pl.BlockSpec((1, GDN_DV), lambda b, c: (0, 0)),
        ],
        out_specs=(
            pl.BlockSpec((C, vdim), lambda b, c: (b * n + c, 0)),
            pl.BlockSpec((1, nvh, GDN_DK, GDN_DV), lambda b, c: (b, 0, 0, 0)),
        ),
        scratch_shapes=[pltpu.VMEM((nvh, GDN_DK, GDN_DV), F32), pltpu.VMEM((nvh, C, GDN_DV), F32)],
        compiler_params=_cparams(("parallel", "arbitrary")),
        name=name,
    )(qh, kh, vh, bb, gcb, grow, proj, s0, norm_g.reshape(1, GDN_DV))


def _router_body(x_ref, g_ref, wh_ref, wl_ref, b_ref, ri_ref, rw_ref):
    xn = _rms(x_ref[...], g_ref[...])
    xh, xl = _split2(xn)
    logits = _dot(xh, wh_ref[...]) + (_dot(xh, wl_ref[...]) + _dot(xl, wh_ref[...])) + b_ref[...]
    lane = lax.broadcasted_iota(jnp.int32, logits.shape, 1)
    big = jnp.int32(1 << 20)
    gl = jnp.where(lane < N_GROUPS, logits, NEG_BIG)
    gm = jnp.max(gl, axis=-1, keepdims=True)
    gi = jnp.min(jnp.where(gl == gm, lane, big), axis=-1, keepdims=True)
    gw = 1.0 / jnp.sum(jnp.exp(gl - gm), axis=-1, keepdims=True)
    in_grp = (lane >= N_GROUPS) & (lane < N_GROUPS * (EXPERTS_PER_GROUP + 1)) & ((lane // EXPERTS_PER_GROUP) == gi + 1)
    el = jnp.where(in_grp, logits, NEG_BIG)
    m1 = jnp.max(el, axis=-1, keepdims=True)
    i1 = jnp.min(jnp.where(el == m1, lane, big), axis=-1, keepdims=True)
    el2 = jnp.where(lane == i1, NEG_BIG, el)
    m2 = jnp.max(el2, axis=-1, keepdims=True)
    i2 = jnp.min(jnp.where(el2 == m2, lane, big), axis=-1, keepdims=True)
    p2 = jnp.exp(m2 - m1)
    w1 = gw / (1.0 + p2)
    w2 = gw * p2 / (1.0 + p2)
    ri_ref[...] = jnp.where(lane == 0, i1 - N_GROUPS, jnp.where(lane == 1, i2 - N_GROUPS, 0))
    rw_ref[...] = jnp.where(lane == 0, w1, jnp.where(lane == 1, w2, 0.0))


def _router(h, g, wh, wl, b, *, bm, name):
    M, D = h.shape
    return pl.pallas_call(
        _router_body,
        out_shape=(jax.ShapeDtypeStruct((M, LANES), jnp.int32), jax.ShapeDtypeStruct((M, LANES), F32)),
        grid=(M // bm,),
        in_specs=[
            pl.BlockSpec((bm, D), lambda i: (i, 0)),
            pl.BlockSpec((1, D), lambda i: (0, 0)),
            pl.BlockSpec((D, LANES), lambda i: (0, 0)),
            pl.BlockSpec((D, LANES), lambda i: (0, 0)),
            pl.BlockSpec((1, LANES), lambda i: (0, 0)),
        ],
        out_specs=(pl.BlockSpec((bm, LANES), lambda i: (i, 0)), pl.BlockSpec((bm, LANES), lambda i: (i, 0))),
        compiler_params=_cparams(("parallel",)),
        name=name,
    )(h, g.reshape(1, D), wh, wl, b)


def _row_copy(src, s, dst, d, sem):
    return pltpu.make_async_copy(src.at[pl.ds(s, 1)], dst.at[pl.ds(d, 1)], sem)


def _scatter_body(dest_ref, h_ref, g_ref, xs_in_ref, xs_ref, xn_scr, sem):
    del xs_in_ref
    bm = h_ref.shape[0]
    base = pl.program_id(0) * (2 * bm)
    xn_scr[...] = _rms(h_ref[...], g_ref[...])

    def issue(r, c):
        for kk in range(2):
            _row_copy(xn_scr, r, xs_ref, dest_ref[base + 2 * r + kk], sem).start()
        return c

    lax.fori_loop(0, bm, issue, 0)

    def drain(r, c):
        for kk in range(2):
            _row_copy(xn_scr, 0, xs_ref, 0, sem).wait()
        return c

    lax.fori_loop(0, bm, drain, 0)


def _scatter(dest, h, g, xs_zero, *, bm, name):
    M, D = h.shape
    return pl.pallas_call(
        _scatter_body,
        out_shape=jax.ShapeDtypeStruct(xs_zero.shape, F32),
        grid_spec=pltpu.PrefetchScalarGridSpec(
            num_scalar_prefetch=1,
            grid=(M // bm,),
            in_specs=[
                pl.BlockSpec((bm, D), lambda i, d: (i, 0)),
                pl.BlockSpec((1, D), lambda i, d: (0, 0)),
                pl.BlockSpec(memory_space=pl.ANY),
            ],
            out_specs=pl.BlockSpec(memory_space=pl.ANY),
            scratch_shapes=[pltpu.VMEM((bm, D), F32), pltpu.SemaphoreType.DMA(())],
        ),
        input_output_aliases={3: 0},
        compiler_params=_cparams(("arbitrary",)),
        name=name,
    )(dest, h, g.reshape(1, D), xs_zero)


def _expert_body(te_ref, nv_ref, xs_ref, wg_ref, wu_ref, wd_ref, ys_ref, wgb, wub, wdb):
    t = pl.program_id(0)
    changed = (t == 0) | (te_ref[t] != te_ref[jnp.maximum(t - 1, 0)])

    @pl.when(changed)
    def _():
        wgb[...] = wg_ref[0].astype(BF16)
        wub[...] = wu_ref[0].astype(BF16)
        wdb[...] = wd_ref[0].astype(BF16)

    @pl.when(t < nv_ref[0])
    def _():
        x = xs_ref[...].astype(BF16)
        hid = _silu(_dot(x, wgb[...])) * _dot(x, wub[...])
        ys_ref[...] = _dot(hid.astype(BF16), wdb[...])

    @pl.when(t >= nv_ref[0])
    def _():
        ys_ref[...] = jnp.zeros_like(ys_ref)


def _experts(tile_expert, nvalid, xs, wg, wu, wd, *, tm, name):
    R, D = xs.shape
    Fd = wg.shape[2]
    return pl.pallas_call(
        _expert_body,
        out_shape=jax.ShapeDtypeStruct((R, D), F32),
        grid_spec=pltpu.PrefetchScalarGridSpec(
            num_scalar_prefetch=2,
            grid=(R // tm,),
            in_specs=[
                pl.BlockSpec((tm, D), lambda t, te, nv: (t, 0)),
                pl.BlockSpec((1, D, Fd), lambda t, te, nv: (te[t], 0, 0)),
                pl.BlockSpec((1, D, Fd), lambda t, te, nv: (te[t], 0, 0)),
                pl.BlockSpec((1, Fd, D), lambda t, te, nv: (te[t], 0, 0)),
            ],
            out_specs=pl.BlockSpec((tm, D), lambda t, te, nv: (t, 0)),
            scratch_shapes=[pltpu.VMEM((D, Fd), BF16), pltpu.VMEM((D, Fd), BF16), pltpu.VMEM((Fd, D), BF16)],
        ),
        compiler_params=_cparams(("arbitrary",)),
        name=name,
    )(tile_expert, nvalid, xs, wg, wu, wd)


def _combine_body(dest_ref, h_ref, rw_ref, ys_ref, o_ref, y0, y1, sem):
    bm = h_ref.shape[0]
    base = pl.program_id(0) * (2 * bm)

    def issue(r, c):
        _row_copy(ys_ref, dest_ref[base + 2 * r], y0, r, sem).start()
        _row_copy(ys_ref, dest_ref[base + 2 * r + 1], y1, r, sem).start()
        return c

    lax.fori_loop(0, bm, issue, 0)

    def drain(r, c):
        _row_copy(ys_ref, 0, y0, 0, sem).wait()
        _row_copy(ys_ref, 0, y1, 0, sem).wait()
        return c

    lax.fori_loop(0, bm, drain, 0)
    w = rw_ref[...]
    o_ref[...] = h_ref[...] + (w[:, 0:1] * y0[...] + w[:, 1:2] * y1[...])


def _combine(dest, h, rw, ys, *, bm, name):
    M, D = h.shape
    return pl.pallas_call(
        _combine_body,
        out_shape=jax.ShapeDtypeStruct((M, D), F32),
        grid_spec=pltpu.PrefetchScalarGridSpec(
            num_scalar_prefetch=1,
            grid=(M // bm,),
            in_specs=[
                pl.BlockSpec((bm, D), lambda i, d: (i, 0)),
                pl.BlockSpec((bm, LANES), lambda i, d: (i, 0)),
                pl.BlockSpec(memory_space=pl.ANY),
            ],
            out_specs=pl.BlockSpec((bm, D), lambda i, d: (i, 0)),
            scratch_shapes=[pltpu.VMEM((bm, D), F32), pltpu.VMEM((bm, D), F32), pltpu.SemaphoreType.DMA(())],
        ),
        compiler_params=_cparams(("arbitrary",)),
        name=name,
    )(dest, h, rw, ys)


def _moe(h, g, w_group, b_group, w_router, b_router, w_gate, w_up, w_down, *, tm, name):
    M, D = h.shape
    n_exp = w_gate.shape[0]
    wr = jnp.zeros((D, LANES), F32).at[:, :N_GROUPS].set(w_group).at[:, N_GROUPS:N_GROUPS + n_exp].set(w_router)
    wr_hi = wr.astype(BF16)
    wr_lo = (wr - wr_hi.astype(F32)).astype(BF16)
    br = jnp.full((1, LANES), NEG_BIG, F32).at[0, :N_GROUPS].set(b_group).at[0, N_GROUPS:N_GROUPS + n_exp].set(b_router)
    ri, rw = _router(h, g, wr_hi, wr_lo, br, bm=512, name=name + "_router")
    e = ri[:, :2].reshape(-1)
    onehot = (e[:, None] == jnp.arange(n_exp, dtype=jnp.int32)[None, :]).astype(jnp.int32)
    cnt = jnp.sum(onehot, axis=0)
    rank = jnp.sum((jnp.cumsum(onehot, axis=0) - onehot) * onehot, axis=1)
    ntile = (cnt + tm - 1) // tm
    tile_end = jnp.cumsum(ntile)
    off = (tile_end - ntile) * tm
    dest = (off[e] + rank).astype(jnp.int32)
    n_tiles = (2 * M) // tm + n_exp
    nvalid = tile_end[-1].astype(jnp.int32)
    tix = jnp.minimum(jnp.arange(n_tiles, dtype=jnp.int32), nvalid - 1)
    tile_expert = jnp.searchsorted(tile_end, tix, side="right").astype(jnp.int32)
    xs = _scatter(dest, h, g, jnp.zeros((n_tiles * tm, D), F32), bm=256, name=name + "_scatter")
    ys = _experts(tile_expert, nvalid.reshape(1), xs, w_gate, w_up, w_down, tm=tm, name=name + "_experts")
    return _combine(dest, h, rw, ys, bm=256, name=name + "_combine")


def _kv_body(x_ref, g_ref, w_ref, gc_ref, g1_ref, g2_ref, cos_ref, sin_ref, ckv_ref, y1_ref, y2_ref, *, rank, half):
    kv = _dot(_rms(x_ref[...], g_ref[...]).astype(BF16), w_ref[...])
    ckv_ref[...] = _rms(kv[:, :rank], gc_ref[...])
    x1 = kv[:, rank:rank + LANES]
    x2 = kv[:, rank + LANES:rank + 2 * LANES]
    ms = (jnp.sum(x1 * x1, axis=-1, keepdims=True) + jnp.sum(x2 * x2, axis=-1, keepdims=True)) * (1.0 / (2 * half))
    r = lax.rsqrt(ms + EPS)
    n1 = x1 * r * g1_ref[...]
    n2 = x2 * r * g2_ref[...]
    cos = cos_ref[...]
    sin = sin_ref[...]
    y1_ref[...] = n1 * cos - n2 * sin
    y2_ref[...] = n2 * cos + n1 * sin


def _kv(h, g, w_pad, g_ckv, g1, g2, cos_p, sin_p, *, rank, half, bm, name):
    M, D = h.shape
    Wn = w_pad.shape[1]
    row = lambda i: (i, 0)
    fix = lambda i: (0, 0)
    return pl.pallas_call(
        functools.partial(_kv_body, rank=rank, half=half),
        out_shape=(
            jax.ShapeDtypeStruct((M, rank), F32),
            jax.ShapeDtypeStruct((M, LANES), F32),
            jax.ShapeDtypeStruct((M, LANES), F32),
        ),
        grid=(M // bm,),
        in_specs=[
            pl.BlockSpec((bm, D), row),
            pl.BlockSpec((1, D), fix),
            pl.BlockSpec((D, Wn), fix),
            pl.BlockSpec((1, rank), fix),
            pl.BlockSpec((1, LANES), fix),
            pl.BlockSpec((1, LANES), fix),
            pl.BlockSpec((bm, LANES), row),
            pl.BlockSpec((bm, LANES), row),
        ],
        out_specs=(pl.BlockSpec((bm, rank), row), pl.BlockSpec((bm, LANES), row), pl.BlockSpec((bm, LANES), row)),
        compiler_params=_cparams(("parallel",)),
        name=name,
    )(h, g.reshape(1, D), w_pad, g_ckv.reshape(1, rank), g1, g2, cos_p, sin_p)


def _keys_body(c_ref, w_ref, gkn_ref, y1_ref, y2_ref, k_ref, v_ref, *, nh, half):
    kk = _dot(c_ref[...].astype(BF16), w_ref[...])
    pe = jnp.concatenate([y1_ref[:, :half], y2_ref[:, :half]], axis=1).astype(BF16)
    gkn = gkn_ref[...]
    for h in range(nh):
        k_ref[h, :, 0:NOPE_DIM] = _rms(kk[:, h * NOPE_DIM:(h + 1) * NOPE_DIM], gkn).astype(BF16)
        k_ref[h, :, NOPE_DIM:NOPE_DIM + 2 * half] = pe
        v_ref[h] = kk[:, nh * NOPE_DIM + h * V_DIM:nh * NOPE_DIM + (h + 1) * V_DIM].astype(BF16)


def _keys(ckv, w_kv, g_kn, y1, y2, *, rows, nh, half, bm, name):
    rank = ckv.shape[1]
    row = lambda i: (i, 0)
    return pl.pallas_call(
        functools.partial(_keys_body, nh=nh, half=half),
        out_shape=(
            jax.ShapeDtypeStruct((nh, rows, NOPE_DIM + 2 * half), BF16),
            jax.ShapeDtypeStruct((nh, rows, V_DIM), BF16),
        ),
        grid=(rows // bm,),
        in_specs=[
            pl.BlockSpec((bm, rank), row),
            pl.BlockSpec(w_kv.shape, lambda i: (0, 0)),
            pl.BlockSpec((1, NOPE_DIM), lambda i: (0, 0)),
            pl.BlockSpec((bm, LANES), row),
            pl.BlockSpec((bm, LANES), row),
        ],
        out_specs=(
            pl.BlockSpec((nh, bm, NOPE_DIM + 2 * half), lambda i: (0, i, 0)),
            pl.BlockSpec((nh, bm, V_DIM), lambda i: (0, i, 0)),
        ),
        compiler_params=_cparams(("parallel",)),
        name=name,
    )(ckv, w_kv, g_kn.reshape(1, NOPE_DIM), y1, y2)


def _q_body(x_ref, g_ref, wa_ref, ga_ref, wb_ref, gn_ref, gp1_ref, gp2_ref, cos_ref, sin_ref, bs_ref, o_ref,
            *, nh, half, scale):
    xn = _rms(x_ref[...], g_ref[...]).astype(BF16)
    qa = _rms(_dot(xn, wa_ref[...]), ga_ref[...]).astype(BF16)
    q = _dot(qa, wb_ref[...])
    n0 = nh * NOPE_DIM
    hw = nh * half
    x1 = q[:, n0:n0 + hw]
    x2 = q[:, n0 + hw:n0 + 2 * hw]
    sq = x1 * x1 + x2 * x2
    sh, sl = _split2(sq)
    ss = _dot(sh, bs_ref[...]) + _dot(sl, bs_ref[...])
    r = lax.rsqrt(ss * (1.0 / (2 * half)) + EPS)
    n1 = x1 * r * gp1_ref[...]
    n2 = x2 * r * gp2_ref[...]
    cos = cos_ref[...]
    sin = sin_ref[...]
    y1 = (n1 * cos - n2 * sin) * scale
    y2 = (n2 * cos + n1 * sin) * scale
    gn = gn_ref[...]
    for h in range(nh):
        o_ref[h, :, 0:NOPE_DIM] = _rms(q[:, h * NOPE_DIM:(h + 1) * NOPE_DIM], gn) * scale
        o_ref[h, :, NOPE_DIM:NOPE_DIM + half] = y1[:, h * half:(h + 1) * half]
        o_ref[h, :, NOPE_DIM + half:NOPE_DIM + 2 * half] = y2[:, h * half:(h + 1) * half]


def _queries(h, g, wa, ga, wb, gn, gp1, gp2, cos_t, sin_t, bsum, *, nh, half, scale, bm, name):
    M, D = h.shape
    qr = wa.shape[1]
    hw = nh * half
    row = lambda i: (i, 0)
    fix = lambda i: (0, 0)
    return pl.pallas_call(
        functools.partial(_q_body, nh=nh, half=half, scale=scale),
        out_shape=jax.ShapeDtypeStruct((nh, M, NOPE_DIM + 2 * half), F32),
        grid=(M // bm,),
        in_specs=[
            pl.BlockSpec((bm, D), row),
            pl.BlockSpec((1, D), fix),
            pl.BlockSpec((D, qr), fix),
            pl.BlockSpec((1, qr), fix),
            pl.BlockSpec(wb.shape, fix),
            pl.BlockSpec((1, NOPE_DIM), fix),
            pl.BlockSpec((1, hw), fix),
            pl.BlockSpec((1, hw), fix),
            pl.BlockSpec((bm, hw), row),
            pl.BlockSpec((bm, hw), row),
            pl.BlockSpec((hw, hw), fix),
        ],
        out_specs=pl.BlockSpec((nh, bm, NOPE_DIM + 2 * half), lambda i: (0, i, 0)),
        compiler_params=_cparams(("parallel",)),
        name=name,
    )(h, g.reshape(1, D), wa, ga.reshape(1, qr), wb, gn.reshape(1, NOPE_DIM), gp1, gp2, cos_t, sin_t, bsum)


def _pattn_body(qi_ref, kj_ref, q_ref, k_ref, v_ref, o_ref, m_scr, l_scr, acc_scr, *, bq, bk):
    t = pl.program_id(1)
    i = qi_ref[t]
    j = kj_ref[t]
    last = ((i + 1) * bq - 1) // bk

    @pl.when(j == 0)
    def _():
        m_scr[...] = jnp.full_like(m_scr, -jnp.inf)
        l_scr[...] = jnp.zeros_like(l_scr)
        acc_scr[...] = jnp.zeros_like(acc_scr)

    def step(masked):
        s = _dot_nt(q_ref[0].astype(BF16), k_ref[0])
        if masked:
            rows = i * bq + lax.broadcasted_iota(jnp.int32, (bq, bk), 0)
            cols = j * bk + lax.broadcasted_iota(jnp.int32, (bq, bk), 1)
            s = jnp.where(cols <= rows, s, NEG_BIG)
        m_prev = m_scr[...]
        m_new = jnp.maximum(m_prev, jnp.max(s, axis=-1, keepdims=True))
        alpha = jnp.exp(m_prev - m_new)
        p = jnp.exp(s - m_new)
        l_scr[...] = alpha * l_scr[...] + jnp.sum(p, axis=-1, keepdims=True)
        acc_scr[...] = alpha * acc_scr[...] + _dot(p.astype(BF16), v_ref[0])
        m_scr[...] = m_new

    diag = (j + 1) * bk - 1 > i * bq

    @pl.when(diag)
    def _():
        step(True)

    @pl.when(jnp.logical_not(diag))
    def _():
        step(False)

    @pl.when(j == last)
    def _():
        o_ref[...] = (acc_scr[...] / l_scr[...]).astype(o_ref.dtype)


def _prompt_attention(qf, kf, vf, *, rows, bq, bk, name):
    nh, _, dk = kf.shape
    pairs = [(i, j) for i in range(rows // bq) for j in range(((i + 1) * bq - 1) // bk + 1)]
    qi = jnp.asarray(np.array([p[0] for p in pairs], np.int32))
    kj = jnp.asarray(np.array([p[1] for p in pairs], np.int32))
    return pl.pallas_call(
        functools.partial(_pattn_body, bq=bq, bk=bk),
        out_shape=jax.ShapeDtypeStruct((rows, nh * V_DIM), BF16),
        grid_spec=pltpu.PrefetchScalarGridSpec(
            num_scalar_prefetch=2,
            grid=(nh, len(pairs)),
            in_specs=[
                pl.BlockSpec((1, bq, dk), lambda h, t, qi, kj: (h, qi[t], 0)),
                pl.BlockSpec((1, bk, dk), lambda h, t, qi, kj: (h, kj[t], 0)),
                pl.BlockSpec((1, bk, V_DIM), lambda h, t, qi, kj: (h, kj[t], 0)),
            ],
            out_specs=pl.BlockSpec((bq, V_DIM), lambda h, t, qi, kj: (qi[t], h)),
            scratch_shapes=[pltpu.VMEM((bq, 1), F32), pltpu.VMEM((bq, 1), F32), pltpu.VMEM((bq, V_DIM), F32)],
        ),
        compiler_params=_cparams(("parallel", "arbitrary")),
        name=name,
    )(qi, kj, qf, kf, vf)


def _sattn_body(pt_ref, q_ref, wnt_ref, wuv_ref, gkn_ref, cn_ref, y1_ref, y2_ref, *rest, PP, nh, nq, half):
    ck_refs = rest[:PP]
    kp_refs = rest[PP:2 * PP]
    o_ref = rest[2 * PP]
    a_scr, qpe_scr, cpad_scr, ppad_scr, m_scr, l_scr, acc_scr = rest[2 * PP + 1:]
    j = pl.program_id(1)
    R = nh * nq
    rank = cn_ref.shape[1]

    @pl.when(j == 0)
    def _():
        gkn = gkn_ref[...]
        for h in range(nh):
            qh = q_ref[h]
            qn = (qh[:, :NOPE_DIM] * gkn).astype(BF16)
            a_scr[h * nq:(h + 1) * nq, :] = _dot(qn, wnt_ref[h * NOPE_DIM:(h + 1) * NOPE_DIM, :])
            qpe_scr[h * nq:(h + 1) * nq, :] = qh[:, NOPE_DIM:NOPE_DIM + 2 * half]
        m_scr[...] = jnp.full_like(m_scr, -jnp.inf)
        l_scr[...] = jnp.zeros_like(l_scr)
        acc_scr[...] = jnp.zeros_like(acc_scr)

    a = a_scr[...].astype(BF16)
    qpe = qpe_scr[...].astype(BF16)
    wnt = wnt_ref[...]

    def scores(cb, kpb):
        n = cb.shape[0]
        kT = _dot_nt(wnt, cb)
        k4 = kT.reshape(nh, NOPE_DIM // SUBLANES, SUBLANES, n)
        tot = jnp.sum(jnp.sum(k4 * k4, axis=1), axis=1, keepdims=True)
        rinv = lax.rsqrt(tot * (1.0 / NOPE_DIM) + EPS)
        num = _dot_nt(a, cb).reshape(nh, nq, n)
        pe = _dot_nt(qpe, kpb).reshape(nh, nq, n)
        return (num * rinv + pe).reshape(R, n)

    def update(s_parts, c_parts):
        m_prev = m_scr[...]
        m_new = m_prev
        for s in s_parts:
            m_new = jnp.maximum(m_new, jnp.max(s, axis=-1, keepdims=True))
        alpha = jnp.exp(m_prev - m_new)
        l_new = alpha * l_scr[...]
        acc = alpha * acc_scr[...]
        for s, cb in zip(s_parts, c_parts):
            p = jnp.exp(s - m_new)
            l_new = l_new + jnp.sum(p, axis=-1, keepdims=True)
            acc = acc + _dot(p.astype(BF16), cb)
        m_scr[...] = m_new
        l_scr[...] = l_new
        acc_scr[...] = acc

    s_parts = []
    c_parts = []
    for p in range(PP):
        cb = ck_refs[p][0].astype(BF16)
        s_parts.append(scores(cb, kp_refs[p][0].astype(BF16)))
        c_parts.append(cb)
    update(s_parts, c_parts)

    @pl.when(j == pl.num_programs(1) - 1)
    def _():
        cpad_scr[...] = jnp.zeros_like(cpad_scr)
        ppad_scr[...] = jnp.zeros_like(ppad_scr)
        cpad_scr[0:nq, :] = cn_ref[...]
        ppad_scr[0:nq, :] = jnp.concatenate([y1_ref[:, :half], y2_ref[:, :half]], axis=1)
        cb = cpad_scr[...].astype(BF16)
        s = scores(cb, ppad_scr[...].astype(BF16))
        qrow = lax.broadcasted_iota(jnp.int32, s.shape, 0) % nq
        col = lax.broadcasted_iota(jnp.int32, s.shape, 1)
        s = jnp.where(col <= qrow, s, NEG_BIG)
        update([s], [cb])
        lat = (acc_scr[...] / l_scr[...]).astype(BF16)
        for h in range(nh):
            o_ref[:, h * V_DIM:(h + 1) * V_DIM] = _dot(lat[h * nq:(h + 1) * nq, :],
                                                        wuv_ref[:, h * V_DIM:(h + 1) * V_DIM]).astype(o_ref.dtype)


def _sample_attention(page_table, qf, wnt, wuv, g_kn, ckv, y1, y2, cache_ckv, cache_kpe, *, row0, nq, PP, half, name):
    B, n_pages = page_table.shape
    nh = qf.shape[0]
    dk = qf.shape[2]
    rank = ckv.shape[1]
    rb0 = row0 // nq
    R = nh * nq
    fix = lambda b, j, pt: (0, 0)
    new = lambda b, j, pt: (rb0 + b, 0)
    in_specs = [
        pl.BlockSpec((nh, nq, dk), lambda b, j, pt: (0, rb0 + b, 0)),
        pl.BlockSpec(wnt.shape, fix),
        pl.BlockSpec(wuv.shape, fix),
        pl.BlockSpec((1, NOPE_DIM), fix),
        pl.BlockSpec((nq, rank), new),
        pl.BlockSpec((nq, LANES), new),
        pl.BlockSpec((nq, LANES), new),
    ]
    args = [qf, wnt, wuv, g_kn.reshape(1, NOPE_DIM), ckv, y1, y2]
    for p in range(PP):
        in_specs.append(pl.BlockSpec((1, PAGE_SIZE, rank), lambda b, j, pt, p=p: (pt[b, j * PP + p], 0, 0)))
        args.append(cache_ckv)
    for p in range(PP):
        in_specs.append(pl.BlockSpec((1, PAGE_SIZE, 2 * half), lambda b, j, pt, p=p: (pt[b, j * PP + p], 0, 0)))
        args.append(cache_kpe)
    return pl.pallas_call(
        functools.partial(_sattn_body, PP=PP, nh=nh, nq=nq, half=half),
        out_shape=jax.ShapeDtypeStruct((B * nq, nh * V_DIM), F32),
        grid_spec=pltpu.PrefetchScalarGridSpec(
            num_scalar_prefetch=1,
            grid=(B, n_pages // PP),
            in_specs=in_specs,
            out_specs=pl.BlockSpec((nq, nh * V_DIM), lambda b, j, pt: (b, 0)),
            scratch_shapes=[
                pltpu.VMEM((R, rank), F32),
                pltpu.VMEM((R, 2 * half), F32),
                pltpu.VMEM((PAGE_SIZE, rank), F32),
                pltpu.VMEM((PAGE_SIZE, 2 * half), F32),
                pltpu.VMEM((R, 1), F32),
                pltpu.VMEM((R, 1), F32),
                pltpu.VMEM((R, rank), F32),
            ],
        ),
        compiler_params=_cparams(("parallel", "arbitrary")),
        name=name,
    )(page_table, *args)


def kernel(x_prompt, x_sample, cache_ckv, cache_kpe, state_gdn, state_conv, page_table, p_prompt, p_sample, g_attn, g_ffn, g_ple, w_gdn_in, gdn_conv_w, gdn_a_log, gdn_dt_bias, gdn_norm_g, w_gdn_out, g_kv_in, w_kva, g_ckv, g_kpe, w_kvb, g_kn, w_qa, g_qa, w_qb, g_qn, g_qp, w_mla_o, w_group, b_group, w_router, b_router, w_exp_gate, w_exp_up, w_exp_down, w_ple_gate, w_ple_proj):
    Bp, Tp, D = x_prompt.shape
    Bs, Ts, _ = x_sample.shape
    assert Bp == 1 and Ts == SUBLANES
    Ns = Bs * Ts
    N = Tp + Ns
    nvh = state_gdn.shape[2]
    qk_dim = (gdn_conv_w.shape[2] - nvh * GDN_DV) // 2
    nqh = qk_dim // GDN_DK
    v_dim = nvh * GDN_DV
    conv_dim = 2 * qk_dim + v_dim
    n_mla = w_kvb.shape[1]
    rank = w_kvb.shape[0]
    half = ROPE_DIM // 2
    past_len = page_table.shape[1] * PAGE_SIZE
    mla_scale = float((NOPE_DIM + ROPE_DIM) ** -0.5)

    h = jnp.concatenate([x_prompt.reshape(Tp, D), x_sample.reshape(Ns, D)], axis=0)
    pp = jnp.concatenate([p_prompt.reshape(p_prompt.shape[0], Tp, -1), p_sample.reshape(p_sample.shape[0], Ns, -1)], axis=1)

    w_in = w_gdn_in[0]
    w_main = w_in[:, :conv_dim + v_dim].astype(BF16)
    w_ba = jnp.zeros((D, LANES), F32).at[:, :2 * nvh].set(w_in[:, conv_dim + v_dim:]).astype(BF16)
    proj = _mm(h, w_main, g=g_attn[0], bm=1024, bn=512, name="gdn_in_proj")
    ba = _mm(h, w_ba, g=g_attn[0], bm=1024, bn=LANES, name="gdn_in_ba")
    al_row = jnp.zeros((1, LANES), F32).at[0, nvh:2 * nvh].set(gdn_a_log[0])
    dtb_row = jnp.zeros((1, LANES), F32).at[0, nvh:2 * nvh].set(gdn_dt_bias[0])
    conv_w = gdn_conv_w[0]
    buf8_p = jnp.zeros((SUBLANES, conv_dim), F32)
    buf8_s = jnp.concatenate([jnp.zeros((Bs, SUBLANES - 3, conv_dim), F32), state_conv[0]], axis=1).reshape(Bs * SUBLANES, conv_dim)
    qscale = float(GDN_DK ** -0.5)

    def gdn_inputs(row0, rows, C, bg_bm, Bb, L, buf8, halo_src, tag):
        bb, gcb, grow = _bg(ba, al_row, dtb_row, row0=row0, rows=rows, C=C, bm=bg_bm, nh=nvh, name="gdn_bg_" + tag)
        common = dict(row0=row0, rows=rows, Bb=Bb, L=L, nh=8, buf8=buf8)
        qh = _conv(proj, halo_src, conv_w[:, :qk_dim], col0=0, cols=qk_dim, normalize=True, scale=qscale,
                   name="gdn_conv_q_" + tag, **common)
        kh = _conv(proj, None if halo_src is None else halo_src[:, qk_dim:2 * qk_dim], conv_w[:, qk_dim:2 * qk_dim],
                   col0=qk_dim, cols=qk_dim, normalize=True, scale=1.0, name="gdn_conv_k_" + tag,
                   **{**common, "buf8": None if buf8 is None else buf8[:, qk_dim:2 * qk_dim]})
        vh = _conv(proj, None if halo_src is None else halo_src[:, 2 * qk_dim:], conv_w[:, 2 * qk_dim:],
                   col0=2 * qk_dim, cols=v_dim, normalize=False, scale=1.0, name="gdn_conv_v_" + tag,
                   **{**common, "buf8": None if buf8 is None else buf8[:, 2 * qk_dim:]})
        return qh, kh, vh, bb, gcb, grow

    Cp = min(GDN_CHUNK, Tp)
    ins_p = gdn_inputs(0, Tp, Cp, 256, 1, 512, buf8_p, None, "p")
    o_p, S_p = _gdn(*ins_p, proj, jnp.zeros((1, nvh, GDN_DK, GDN_DV), F32), gdn_norm_g[0],
                    B=1, n=Tp // Cp, C=Cp, zrow0=0, out_dtype=BF16, name="gdn_chunks_p")
    ins_s = gdn_inputs(Tp, Ns, Ts, 128, 16, Ts, None, buf8_s, "s")
    o_s, S_s = _gdn(*ins_s, proj, state_gdn[0], gdn_norm_g[0],
                    B=Bs, n=1, C=Ts, zrow0=Tp, out_dtype=F32, name="gdn_chunks_s")
    o_all = jnp.concatenate([o_p, o_s.astype(BF16)], axis=0)
    h = _mm(o_all, w_gdn_out[0].astype(BF16), res=h, bm=1024, bn=512, name="gdn_out_proj")
    new_conv_prompt = proj[Tp - 3:Tp, :conv_dim].reshape(1, 1, 3, conv_dim)
    new_conv_sample = proj[Tp:, :conv_dim].reshape(Bs, Ts, conv_dim)[:, Ts - 3:, :].reshape(1, Bs, 3, conv_dim)

    def channel_mix(h, i):
        h = _moe(h, g_ffn[i], w_group[i], b_group[i], w_router[i], b_router[i],
                 w_exp_gate[i], w_exp_up[i], w_exp_down[i], tm=256, name="moe%d" % i)
        return _ple(h, g_ple[i], w_ple_gate[i].astype(BF16), pp[i], w_ple_proj[i].astype(BF16),
                    bm=1024, bn=512, name="ple%d" % i)

    h = channel_mix(h, 0)

    pos = jnp.concatenate([jnp.arange(Tp), jnp.tile(past_len + jnp.arange(Ts), Bs)])
    inv_freq = ROPE_THETA ** (-jnp.arange(half, dtype=F32) / half)
    ang = pos.astype(F32)[:, None] * inv_freq[None, :]
    cos, sin = jnp.cos(ang), jnp.sin(ang)
    lane_pad = lambda a: jnp.zeros(a.shape[:-1] + (LANES,), F32).at[..., :a.shape[-1]].set(a)
    w_kva_pad = jnp.concatenate([w_kva[:, :rank], lane_pad(w_kva[:, rank:rank + half]), lane_pad(w_kva[:, rank + half:])], axis=1).astype(BF16)
    ckv, y1, y2 = _kv(h, g_kv_in, w_kva_pad, g_ckv, lane_pad(g_kpe[None, :half]), lane_pad(g_kpe[None, half:]),
                      lane_pad(cos), lane_pad(sin), rank=rank, half=half, bm=512, name="mla_kv")
    kpe = jnp.concatenate([y1[:, :half], y2[:, :half]], axis=1)

    wq = w_qb[0]
    wq_cols = jnp.concatenate([
        wq[:, :, :NOPE_DIM].reshape(wq.shape[0], -1),
        wq[:, :, NOPE_DIM:NOPE_DIM + half].reshape(wq.shape[0], -1),
        wq[:, :, NOPE_DIM + half:].reshape(wq.shape[0], -1)], axis=1).astype(BF16)
    hw = n_mla * half
    lane_head = jnp.arange(hw) // half
    bsum = (lane_head[:, None] == lane_head[None, :]).astype(BF16)
    qf = _queries(h, g_attn[1], w_qa[0].astype(BF16), g_qa[0], wq_cols, g_qn[0],
                  jnp.tile(g_qp[0, :half], n_mla)[None], jnp.tile(g_qp[0, half:], n_mla)[None],
                  jnp.tile(cos, (1, n_mla)), jnp.tile(sin, (1, n_mla)), bsum,
                  nh=n_mla, half=half, scale=mla_scale, bm=256, name="mla_q")
    w_nope = w_kvb[:, :, :NOPE_DIM].reshape(rank, n_mla * NOPE_DIM)
    w_uv = w_kvb[:, :, NOPE_DIM:].reshape(rank, n_mla * V_DIM)
    kf, vf = _keys(ckv, jnp.concatenate([w_nope, w_uv], axis=1).astype(BF16), g_kn, y1, y2,
                   rows=Tp, nh=n_mla, half=half, bm=512, name="mla_keys")
    ao_p = _prompt_attention(qf, kf, vf, rows=Tp, bq=1024, bk=512, name="mla_prompt_attn")
    ao_s = _sample_attention(page_table, qf, w_nope.T.astype(BF16), w_uv.astype(BF16), g_kn, ckv, y1, y2,
                             cache_ckv, cache_kpe, row0=Tp, nq=Ts, PP=8, half=half, name="mla_sample_attn")
    ao = jnp.concatenate([ao_p, ao_s.astype(BF16)], axis=0)
    h = _mm(ao, w_mla_o[0].astype(BF16), res=h, bm=1024, bn=512, name="mla_out_proj")
    h = channel_mix(h, 1)

    return (
        h[:Tp].reshape(1, Tp, D),
        h[Tp:].reshape(Bs, Ts, D),
        ckv[:Tp].reshape(1, Tp, rank),
        kpe[:Tp].reshape(1, Tp, 2 * half),
        ckv[Tp:].reshape(Bs, Ts, rank),
        kpe[Tp:].reshape(Bs, Ts, 2 * half),
        S_p.reshape(1, 1, nvh, GDN_DK, GDN_DV),
        new_conv_prompt,
        S_s.reshape(1, Bs, nvh, GDN_DK, GDN_DV),
        new_conv_sample,
    )
```
